```python
import math
import jax, jax.numpy as jnp
from jax import lax
import numpy as np

D_MODEL = 2048
BATCH = 4
SEQ = 4096
DEPTH = 2

GRID_W = 64
CTX_LEN = 256
N_MOD = 6
NORM_EPS = 1e-6

A_DIM = 3 * D_MODEL // 8
A_HEAD_DIM = 64
A_HEADS = A_DIM // A_HEAD_DIM
A_DECAY_RANK = 64
A_ICLR_RANK = 64
A_GATE_RANK = 128
A_DECAY_SCALE = math.exp(-0.5)
A_GN_EPS = 64e-5

B_DIM = 3 * D_MODEL // 8
B_HEADS = 4
B_KEY_DIM = B_DIM // 2
B_HEAD_K = B_KEY_DIM // B_HEADS
B_HEAD_V = B_DIM // B_HEADS
B_GATE_RANK = 16
GLA_TAU = 16.0
GLA_CHUNK = 64

C_DIM = D_MODEL // 4
C_GROUP = 16
C_GROUPS = C_DIM // C_GROUP
C_STATE = 64
S5_DT_MIN = 1e-3
S5_DT_MAX = 1e-1

N_BRANCH = 3
CONV_DIM = 3 * A_DIM + 2 * B_KEY_DIM + B_DIM
IN_SPLITS = (A_DIM, A_DIM, A_DIM, B_KEY_DIM, B_KEY_DIM, B_DIM,
             2 * A_DECAY_RANK, 2 * A_ICLR_RANK, A_GATE_RANK, 2 * B_GATE_RANK,
             B_DIM, C_DIM, N_BRANCH * D_MODEL)
IN_DIM = sum(IN_SPLITS)
IN_OFFSETS = tuple(int(o) for o in np.cumsum(IN_SPLITS)[:-1])

N_EXPERTS = 16
N_EXPERT_GROUPS = 4
EXPERTS_PER_GROUP = N_EXPERTS // N_EXPERT_GROUPS
TOP_K = 2
D_EXPERT = D_MODEL // 2

kernel_name = "hybrid_rwkv7_gla_s5_moe_prefix_dit"


def rmsnorm(x, g):
    xf = x.astype(jnp.float32)
    y = xf * lax.rsqrt(jnp.mean(xf * xf, axis=-1, keepdims=True) + NORM_EPS)
    return y.astype(x.dtype) * g


def grid_dwconv(z, w, rows, cols):
    nb, n, ch = z.shape
    out = lax.conv_general_dilated(z.reshape(nb, rows, cols, ch), w[:, :, None, :], (1, 1), "SAME",
                                   dimension_numbers=("NHWC", "HWIO", "NHWC"), feature_group_count=ch)
    return out.reshape(nb, n, ch)


def orient(z):
    return jnp.stack([z[0], jnp.flip(z[1], axis=1)])


def both(z):
    return jnp.stack([z, jnp.flip(z, axis=1)])


def in_features(h, w_in, conv_w, rows, cols):
    p = h @ w_in
    p = jnp.concatenate([grid_dwconv(p[..., :CONV_DIM], conv_w, rows, cols), p[..., CONV_DIM:]], axis=-1)
    return jnp.split(p, IN_OFFSETS, axis=-1)


def rwkv7_branch(r, k, v, d_w, d_a, d_g, S0, w0, w2, a0, a2, g2, k_k, k_a, r_k, gn_w, gn_b, need_out):
    f32 = jnp.float32
    hd = lambda z: z.reshape(z.shape[:-1] + (A_HEADS, A_HEAD_DIM))
    tm = lambda z: jnp.moveaxis(z.astype(f32), 2, 0)
    w = jnp.exp(-A_DECAY_SCALE * jax.nn.sigmoid(
        (w0[:, None, None, :] + jnp.einsum("btdr,drc->dbtc", jnp.tanh(d_w), w2)).astype(f32)))
    a = jax.nn.sigmoid((a0[:, None, None, :] + jnp.einsum("btdr,drc->dbtc", d_a, a2)).astype(f32))
    kk = hd(k * k_k).astype(f32)
    kk = kk * lax.rsqrt(jnp.sum(kk * kk, axis=-1, keepdims=True) + 1e-12)
    kd = k.astype(f32)[None] * (1.0 + (a - 1.0) * k_a.astype(f32))
    xs = (tm(both(hd(r))), tm(orient(hd(w))), tm(orient(hd(kd))), tm(both(hd(v))),
          tm(both(kk)), tm(orient(hd(a))))

    def step(S, inp):
        r_t, w_t, k_t, v_t, kk_t, a_t = inp
        sa = jnp.einsum("dbhvk,dbhk->dbhv", S, kk_t)
        S = (S * w_t[..., None, :] - jnp.einsum("dbhv,dbhk->dbhvk", sa, kk_t * a_t)
             + jnp.einsum("dbhv,dbhk->dbhvk", v_t, k_t))
        return S, (jnp.einsum("dbhvk,dbhk->dbhv", S, r_t) if need_out else None)

    S_fin, y = lax.scan(step, S0, xs)
    if not need_out:
        return None, S_fin
    y = orient(jnp.moveaxis(y, 0, 2))
    mu = jnp.mean(y, axis=-1, keepdims=True)
    var = jnp.mean(jnp.square(y - mu), axis=-1, keepdims=True)
    y = (y - mu) * lax.rsqrt(var + A_GN_EPS) * hd(gn_w) + hd(gn_b)
    rf, vf = hd(r).astype(f32), hd(v).astype(f32)
    bonus = jnp.sum(rf[None] * hd(kd) * hd(r_k), axis=-1, keepdims=True) * vf[None]
    y = jnp.sum(y + bonus, axis=0).reshape(r.shape).astype(r.dtype)
    return y * (jax.nn.sigmoid(d_g) @ g2), S_fin


def gla_chunked(q, k, v, g, S0, need_out):
    nd, nb, t, nh, _ = q.shape
    nc = t // GLA_CHUNK
    blk = lambda z: z.reshape(nd, nb, nc, GLA_CHUNK, nh, z.shape[-1])
    q, k, v, g = blk(q), blk(k), blk(v), blk(g)
    cg = jnp.cumsum(g, axis=3)
    tot = cg[:, :, :, -1]
    U = jnp.einsum("dbnihk,dbnihv->dbnhkv", k * jnp.exp(tot[:, :, :, None] - cg), v)

    def step(S, inp):
        log_a, u_n = inp
        return jnp.exp(log_a)[..., None] * S + u_n, S

    S_fin, S_start = lax.scan(step, S0, (jnp.moveaxis(tot, 2, 0), jnp.moveaxis(U, 2, 0)))
    if not need_out:
        return None, S_fin
    S_start = jnp.moveaxis(S_start, 0, 2)
    inter = jnp.einsum("dbnihk,dbnhkv->dbnihv", q * jnp.exp(cg), S_start)
    ref = cg[:, :, :, GLA_CHUNK // 2 - 1:GLA_CHUNK // 2]
    scores = jnp.einsum("dbnihk,dbnjhk->dbnhij", q * jnp.exp(cg - ref), k * jnp.exp(ref - cg))
    mask = jnp.tril(jnp.ones((GLA_CHUNK, GLA_CHUNK), dtype=bool))
    intra = jnp.einsum("dbnhij,dbnjhv->dbnihv", jnp.where(mask, scores, 0.0), v)
    return (inter + intra).reshape(nd, nb, t, nh, v.shape[-1]), S_fin


def gla_branch(q, k, v, og, d_a, S0, a2, ab, norm_g, need_out):
    f32 = jnp.float32
    hk = lambda z: z.reshape(z.shape[:-1] + (B_HEADS, B_HEAD_K))
    hv = lambda z: z.reshape(z.shape[:-1] + (B_HEADS, B_HEAD_V))
    g = jax.nn.log_sigmoid((jnp.einsum("btdr,drc->dbtc", d_a, a2) + ab[:, None, None, :]).astype(f32)) / GLA_TAU
    qh = both(hk(q).astype(f32) * (B_HEAD_K ** -0.5))
    kh = both(hk(k).astype(f32))
    vh = both(hv(v).astype(f32))
    o, S_fin = gla_chunked(qh, kh, vh, orient(hk(g)), S0, need_out)
    if not need_out:
        return None, S_fin
    o = rmsnorm(jnp.sum(orient(o), axis=0), norm_g)
    return o.reshape(v.shape).astype(v.dtype) * jax.nn.silu(og), S_fin


def s5_discretise(lam_re, lam_im, log_dt, b_re, b_im):
    f32 = jnp.float32
    lr, li = lam_re.astype(f32), lam_im.astype(f32)
    dt = jnp.exp(log_dt.astype(f32))[..., None]
    mag = jnp.exp(lr * dt)
    ar, ai = mag * jnp.cos(li * dt), mag * jnp.sin(li * dt)
    den = lr * lr + li * li
    fr = ((ar - 1.0) * lr + ai * li) / den
    fi = (ai * lr - (ar - 1.0) * li) / den
    br, bi = b_re.astype(f32), b_im.astype(f32)
    bbr = fr[..., None] * br - fi[..., None] * bi
    bbi = fr[..., None] * bi + fi[..., None] * br
    return ar, ai, bbr, bbi


def s5_combine(e1, e2):
    a1r, a1i, b1r, b1i = e1
    a2r, a2i, b2r, b2i = e2
    return (a1r * a2r - a1i * a2i, a1r * a2i + a1i * a2r,
            a2r * b1r - a2i * b1i + b2r, a2r * b1i + a2i * b1r + b2i)


def s5_branch(u, h0r, h0i, lam_re, lam_im, log_dt, b_re, b_im, c_re, c_im, d_skip, glu_w, need_out):
    f32 = jnp.float32
    ug = u.astype(f32).reshape(u.shape[:-1] + (C_GROUPS, C_GROUP))
    ar, ai, bbr, bbi = s5_discretise(lam_re, lam_im, log_dt, b_re, b_im)
    bur = orient(jnp.einsum("dgpc,btgc->dbtgp", bbr, ug))
    bui = orient(jnp.einsum("dgpc,btgc->dbtgp", bbi, ug))
    bur = bur.at[:, :, 0].add(ar[:, None] * h0r - ai[:, None] * h0i)
    bui = bui.at[:, :, 0].add(ar[:, None] * h0i + ai[:, None] * h0r)
    shp = bur.shape
    _, _, xr, xi = lax.associative_scan(
        s5_combine, (jnp.broadcast_to(ar[:, None, None], shp), jnp.broadcast_to(ai[:, None, None], shp), bur, bui),
        axis=2)
    hr, hi = xr[:, :, -1], xi[:, :, -1]
    if not need_out:
        return None, hr, hi
    xr, xi = jnp.sum(orient(xr), axis=0), jnp.sum(orient(xi), axis=0)
    y = jnp.einsum("gcp,btgp->btgc", c_re, xr) - jnp.einsum("gcp,btgp->btgc", c_im, xi)
    y = jax.nn.gelu(y.reshape(u.shape).astype(u.dtype) + d_skip * u)
    ya, yg = jnp.split(y @ glu_w, 2, axis=-1)
    return ya * jax.nn.sigmoid(yg), hr, hi


def merge(ya, yb, yc, gates, pa, pb, pc, wo):
    ga, gb, gc = jnp.split(jax.nn.sigmoid(gates), N_BRANCH, axis=-1)
    return (ga * (ya @ pa) + gb * (yb @ pb) + gc * (yc @ pc)) @ wo


def moe_ffn(h, router_w, router_bias, w1, w3, w2):
    f32 = jnp.float32
    scores = jax.nn.sigmoid((h @ router_w).astype(f32))
    biased = scores + router_bias.astype(f32)
    grp = biased.reshape(biased.shape[:-1] + (N_EXPERT_GROUPS, EXPERTS_PER_GROUP))
    gsel = jnp.argmax(jnp.sum(lax.top_k(grp, TOP_K)[0], axis=-1), axis=-1)
    in_grp = (jnp.arange(N_EXPERT_GROUPS) == gsel[..., None])[..., None]
    masked = jnp.where(in_grp, grp, -jnp.inf).reshape(biased.shape)
    _, idx = lax.top_k(masked, TOP_K)
    wsel = jnp.take_along_axis(scores, idx, axis=-1)
    wsel = wsel / jnp.sum(wsel, axis=-1, keepdims=True)
    gate = jnp.sum(jax.nn.one_hot(idx, N_EXPERTS, dtype=f32) * wsel[..., None], axis=-2).astype(h.dtype)
    out = jnp.zeros_like(h)
    for e in range(N_EXPERTS):
        he = jax.nn.silu(h @ w1[e]) * (h @ w3[e])
        out = out + gate[..., e:e + 1] * (he @ w2[e])
    return out


def setup_inputs(seed: int = 0) -> dict:
    key = jax.random.key(seed)
    ks = iter(jax.random.split(key, 64))

    def nrm(shape, scale):
        return scale * jax.random.normal(next(ks), shape, jnp.float32)

    L, D = DEPTH, D_MODEL
    G, P = C_GROUPS, C_STATE
    centre = jnp.zeros((3, 3, 1), jnp.float32).at[1, 1, 0].set(1.0)
    lam_im0 = jnp.pi * jnp.arange(P, dtype=jnp.float32)
    return {
        "x": nrm((BATCH, SEQ, D), 1.0),
        "c": nrm((BATCH, D), 1.0),
        "ctx": nrm((BATCH, CTX_LEN, D), 1.0),
        "c_ctx": nrm((D,), 1.0),
        "mod_w": nrm((L, D, N_MOD * D), 0.5 * D ** -0.5),
        "mod_b": nrm((L, N_MOD * D), 0.02),
        "norm1_g": 1.0 + nrm((L, D), 0.1),
        "norm2_g": 1.0 + nrm((L, D), 0.1),
        "w_in": nrm((L, D, IN_DIM), D ** -0.5),
        "conv_w": centre + nrm((L, 3, 3, CONV_DIM), 0.2),
        "rk_w0": nrm((L, 2, A_DIM), 0.5),
        "rk_w2": nrm((L, 2, A_DECAY_RANK, A_DIM), A_DECAY_RANK ** -0.5),
        "rk_a0": nrm((L, 2, A_DIM), 0.1),
        "rk_a2": nrm((L, 2, A_ICLR_RANK, A_DIM), A_ICLR_RANK ** -0.5),
        "rk_g2": nrm((L, A_GATE_RANK, A_DIM), A_GATE_RANK ** -0.5),
        "rk_kk": 0.85 + nrm((L, A_DIM), 0.1),
        "rk_ka": 1.0 + nrm((L, A_DIM), 0.1),
        "rk_rk": nrm((L, A_DIM), 0.1),
        "rk_gn_w": 1.0 + nrm((L, A_DIM), 0.1),
        "rk_gn_b": nrm((L, A_DIM), 0.02),
        "gla_a2": nrm((L, 2, B_GATE_RANK, B_KEY_DIM), B_GATE_RANK ** -0.5),
        "gla_ab": nrm((L, 2, B_KEY_DIM), 0.1),
        "gla_norm_g": 1.0 + nrm((L, B_HEAD_V), 0.1),
        "s5_lam_re": -0.5 + nrm((L, 2, G, P), 0.01),
        "s5_lam_im": lam_im0 + nrm((L, 2, G, P), 0.01),
        "s5_log_dt": jax.random.uniform(next(ks), (L, 2, G), jnp.float32, math.log(S5_DT_MIN), math.log(S5_DT_MAX)),
        "s5_b_re": nrm((L, G, P, C_GROUP), (2.0 * C_GROUP) ** -0.5),
        "s5_b_im": nrm((L, G, P, C_GROUP), (2.0 * C_GROUP) ** -0.5),
        "s5_c_re": nrm((L, G, C_GROUP, P), (2.0 * P) ** -0.5),
        "s5_c_im": nrm((L, G, C_GROUP, P), (2.0 * P) ** -0.5),
        "s5_d": nrm((L, C_DIM), 0.5),
        "s5_glu_w": nrm((L, C_DIM, 2 * C_DIM), C_DIM ** -0.5),
        "proj_a": nrm((L, A_DIM, D), A_DIM ** -0.5),
        "proj_b": nrm((L, B_DIM, D), B_DIM ** -0.5),
        "proj_c": nrm((L, C_DIM, D), C_DIM ** -0.5),
        "w_out": nrm((L, D, D), D ** -0.5),
        "router_w": nrm((D, N_EXPERTS), D ** -0.5),
        "router_bias": nrm((N_EXPERTS,), 0.01),
        "exp_w1": nrm((L, N_EXPERTS, D, D_EXPERT), D ** -0.5),
        "exp_w3": nrm((L, N_EXPERTS, D, D_EXPERT), D ** -0.5),
        "exp_w2": nrm((L, N_EXPERTS, D_EXPERT, D), D_EXPERT ** -0.5),
        "final_g": 1.0 + nrm((D,), 0.1),
    }


def reference(x, c, ctx, c_ctx, mod_w, mod_b, norm1_g, norm2_g, w_in, conv_w,
              rk_w0, rk_w2, rk_a0, rk_a2, rk_g2, rk_kk, rk_ka, rk_rk, rk_gn_w, rk_gn_b,
              gla_a2, gla_ab, gla_norm_g,
              s5_lam_re, s5_lam_im, s5_log_dt, s5_b_re, s5_b_im, s5_c_re, s5_c_im, s5_d, s5_glu_w,
              proj_a, proj_b, proj_c, w_out, router_w, router_bias, exp_w1, exp_w3, exp_w2, final_g):
    f32 = jnp.float32
    nb, t_lat, _ = x.shape
    t_ctx = ctx.shape[1]
    rows = t_lat // GRID_W
    pair = lambda z: z.reshape(z.shape[:-1] + (2, z.shape[-1] // 2))
    for l in range(DEPTH):
        last = l == DEPTH - 1
        sh1, sc1, ga1, sh2, sc2, ga2 = jnp.split((jax.nn.silu(c) @ mod_w[l] + mod_b[l])[:, None, :], N_MOD, axis=-1)
        csh1, csc1, cga1, csh2, csc2, cga2 = jnp.split(jax.nn.silu(c_ctx) @ mod_w[l] + mod_b[l], N_MOD, axis=-1)
        hl = rmsnorm(x, norm1_g[l]) * (1.0 + sc1) + sh1
        hc = rmsnorm(ctx, norm1_g[l]) * (1.0 + csc1) + csh1
        fc = in_features(hc, w_in[l], conv_w[l], 1, t_ctx)
        fl = in_features(hl, w_in[l], conv_w[l], rows, GRID_W)

        def mix_a(f, S0, need):
            return rwkv7_branch(f[0], f[1], f[2], pair(f[6]), pair(f[7]), f[8], S0,
                                rk_w0[l], rk_w2[l], rk_a0[l], rk_a2[l], rk_g2[l], rk_kk[l], rk_ka[l],
                                rk_rk[l], rk_gn_w[l], rk_gn_b[l], need)

        def mix_b(f, S0, need):
            return gla_branch(f[3], f[4], f[5], f[10], pair(f[9]), S0, gla_a2[l], gla_ab[l], gla_norm_g[l], need)

        def mix_c(f, h0r, h0i, need):
            return s5_branch(f[11], h0r, h0i, s5_lam_re[l], s5_lam_im[l], s5_log_dt[l], s5_b_re[l], s5_b_im[l],
                             s5_c_re[l], s5_c_im[l], s5_d[l], s5_glu_w[l], need)

        ya_c, st_a = mix_a(fc, jnp.zeros((2, nb, A_HEADS, A_HEAD_DIM, A_HEAD_DIM), f32), not last)
        yb_c, st_b = mix_b(fc, jnp.zeros((2, nb, B_HEADS, B_HEAD_K, B_HEAD_V), f32), not last)
        h0 = jnp.zeros((2, nb, C_GROUPS, C_STATE), f32)
        yc_c, st_cr, st_ci = mix_c(fc, h0, h0, not last)
        ya_l, _ = mix_a(fl, st_a, True)
        yb_l, _ = mix_b(fl, st_b, True)
        yc_l, _, _ = mix_c(fl, st_cr, st_ci, True)
        x = x + ga1 * merge(ya_l, yb_l, yc_l, fl[12], proj_a[l], proj_b[l], proj_c[l], w_out[l])
        hl2 = rmsnorm(x, norm2_g[l]) * (1.0 + sc2) + sh2
        if last:
            x = x + ga2 * moe_ffn(hl2, router_w, router_bias, exp_w1[l], exp_w3[l], exp_w2[l])
        else:
            ctx = ctx + cga1 * merge(ya_c, yb_c, yc_c, fc[12], proj_a[l], proj_b[l], proj_c[l], w_out[l])
            hc2 = rmsnorm(ctx, norm2_g[l]) * (1.0 + csc2) + csh2
            y2 = moe_ffn(jnp.concatenate([hc2, hl2], axis=1), router_w, router_bias, exp_w1[l], exp_w3[l], exp_w2[l])
            ctx = ctx + cga2 * y2[:, :t_ctx]
            x = x + ga2 * y2[:, t_ctx:]
    return rmsnorm(x, final_g)
```

```python
import functools
import math

import jax
import jax.numpy as jnp
from jax import lax
from jax.experimental import pallas as pl
from jax.experimental.pallas import tpu as pltpu

F32 = jnp.float32
BF16 = jnp.bfloat16

NORM_EPS = 1e-6
GRID_W = 64
CHUNK = 64
N_MOD = 6
A_HEAD_DIM = 64
A_DECAY_SCALE = math.exp(-0.5)
A_GN_EPS = 64e-5
B_HEADS = 4
GLA_TAU = 16.0
C_GROUP = 16
C_STATE = 64
N_EXPERT_GROUPS = 4
TOP_K = 2
LANE = 128
SUBLANE = 8
MOE_TILE = 256
VMEM_LIMIT = 56 * 1024 * 1024


def _cparams(sem):
    return pltpu.CompilerParams(dimension_semantics=sem, vmem_limit_bytes=VMEM_LIMIT)


def _dot(a, b):
    return jnp.dot(a.astype(BF16), b.astype(BF16), preferred_element_type=F32)


def _dot_nt(a, b):
    return lax.dot_general(a.astype(BF16), b.astype(BF16), (((1,), (1,)), ((), ())),
                           preferred_element_type=F32)


def _dot_tn(a, b):
    return lax.dot_general(a.astype(BF16), b.astype(BF16), (((0,), (0,)), ((), ())),
                           preferred_element_type=F32)


def _split3(x):
    hi = x.astype(BF16)
    r1 = x - hi.astype(F32)
    mid = r1.astype(BF16)
    lo = (r1 - mid.astype(F32)).astype(BF16)
    return hi, mid, lo


def _dot_exact_lhs(m, x):
    mb = m.astype(BF16)
    hi, mid, lo = _split3(x)
    acc = jnp.dot(mb, hi, preferred_element_type=F32)
    acc = acc + jnp.dot(mb, mid, preferred_element_type=F32)
    return acc + jnp.dot(mb, lo, preferred_element_type=F32)


_NN = (((1,), (0,)), ((), ()))
_NT = (((1,), (1,)), ((), ()))
_TN = (((0,), (0,)), ((), ()))


def _dot2(a, b, dims=_NN):
    ah = a.astype(BF16)
    al = (a - ah.astype(F32)).astype(BF16)
    bh = b.astype(BF16)
    bl = (b - bh.astype(F32)).astype(BF16)
    acc = lax.dot_general(ah, bh, dims, preferred_element_type=F32)
    acc = acc + lax.dot_general(ah, bl, dims, preferred_element_type=F32)
    return acc + lax.dot_general(al, bh, dims, preferred_element_type=F32)


def _sigmoid(x):
    return 1.0 / (1.0 + jnp.exp(-x))


def _silu(x):
    return x * _sigmoid(x)


def _pick_tile(n, cands):
    for c in cands:
        if n % c == 0:
            return c
    raise ValueError(f"no tile for {n}")


def _mods_kernel(c_ref, w_ref, b_ref, o_ref):
    a = _silu(c_ref[...])
    o_ref[...] = _dot(a, w_ref[...]) + b_ref[...]


def _mods(cvec, mod_w, mod_b):
    nl, d, n = mod_w.shape
    tn = _pick_tile(n, (1536, 1024, 512, 256, 128))
    return pl.pallas_call(
        _mods_kernel,
        out_shape=jax.ShapeDtypeStruct((nl, SUBLANE, n), F32),
        grid=(nl, n // tn),
        in_specs=[pl.BlockSpec((SUBLANE, d), lambda l, j: (0, 0)),
                  pl.BlockSpec((None, d, tn), lambda l, j: (l, 0, j)),
                  pl.BlockSpec((None, 1, tn), lambda l, j: (l, 0, j))],
        out_specs=pl.BlockSpec((None, SUBLANE, tn), lambda l, j: (l, 0, j)),
        compiler_params=_cparams(("parallel", "parallel")),
        name="mods",
    )(cvec, mod_w, mod_b.reshape(nl, 1, n))


def _mod_spec(layer, tiles_per_batch, n_batch, d):
    def imap(i):
        b = i // tiles_per_batch
        is_ctx = (i % tiles_per_batch) == 0
        return (layer, jnp.where(is_ctx, n_batch, b), 0, 0)
    return pl.BlockSpec((None, None, N_MOD, d), imap)


def _norm_mod(xf, g, shift, scale):
    y = xf * lax.rsqrt(jnp.mean(xf * xf, axis=-1, keepdims=True) + NORM_EPS)
    return y * g * (1.0 + scale) + shift


def _norm1_kernel(x_ref, g_ref, m_ref, h_ref):
    h_ref[...] = _norm_mod(x_ref[...], g_ref[...], m_ref[0:1, :], m_ref[1:2, :]).astype(BF16)


def _norm1(xa, g, mods, layer, geo):
    rows, d = xa.shape
    r = geo["R"]
    return pl.pallas_call(
        _norm1_kernel,
        out_shape=jax.ShapeDtypeStruct((rows, d), BF16),
        grid=(rows // r,),
        in_specs=[pl.BlockSpec((r, d), lambda i: (i, 0)),
                  pl.BlockSpec((1, d), lambda i: (0, 0)),
                  _mod_spec(layer, geo["TPB"], geo["B"], d)],
        out_specs=pl.BlockSpec((r, d), lambda i: (i, 0)),
        compiler_params=_cparams(("parallel",)),
        name="norm1",
    )(xa, g.reshape(1, d), mods)


def _mm_kernel(a_ref, w_ref, o_ref):
    o_ref[...] = jnp.dot(a_ref[...], w_ref[...], preferred_element_type=F32).astype(o_ref.dtype)


def _matmul(a, w, out_dtype, name):
    m, k = a.shape
    n = w.shape[1]
    tm = _pick_tile(m, (1024, 512, 256, 128, 64))
    tn = _pick_tile(n, (512, 256, 128))
    return pl.pallas_call(
        _mm_kernel,
        out_shape=jax.ShapeDtypeStruct((m, n), out_dtype),
        grid=(m // tm, n // tn),
        in_specs=[pl.BlockSpec((tm, k), lambda i, j: (i, 0)),
                  pl.BlockSpec((k, tn), lambda i, j: (0, j))],
        out_specs=pl.BlockSpec((tm, tn), lambda i, j: (i, j)),
        compiler_params=_cparams(("parallel", "parallel")),
        name=name,
    )(a, w)


def _conv_kernel(x_ref, w_ref, o_ref, *, tc_len, n_rows):
    w = w_ref[...]

    def taps(src, dr, first_mask, last_mask):
        left = pltpu.roll(src, 1, 0) * first_mask
        right = pltpu.roll(src, src.shape[0] - 1, 0) * last_mask
        return (left * w[3 * dr:3 * dr + 1, :] + src * w[3 * dr + 1:3 * dr + 2, :]
                + right * w[3 * dr + 2:3 * dr + 3, :])

    tpos = lax.broadcasted_iota(jnp.int32, (tc_len, 1), 0)
    o_ref[0:tc_len, :] = taps(x_ref[0:tc_len, :], 1,
                              (tpos != 0).astype(F32), (tpos != tc_len - 1).astype(F32))

    cpos = lax.broadcasted_iota(jnp.int32, (GRID_W, 1), 0)
    m_first = (cpos != 0).astype(F32)
    m_last = (cpos != GRID_W - 1).astype(F32)

    def row_body(rr, carry):
        base = pl.multiple_of(tc_len + rr * GRID_W, GRID_W)
        cur = x_ref[pl.ds(base, GRID_W), :]
        up_base = pl.multiple_of(tc_len + jnp.maximum(rr - 1, 0) * GRID_W, GRID_W)
        dn_base = pl.multiple_of(tc_len + jnp.minimum(rr + 1, n_rows - 1) * GRID_W, GRID_W)
        up = x_ref[pl.ds(up_base, GRID_W), :] * (rr > 0).astype(F32)
        dn = x_ref[pl.ds(dn_base, GRID_W), :] * (rr < n_rows - 1).astype(F32)
        acc = taps(up, 0, m_first, m_last) + taps(cur, 1, m_first, m_last) + taps(dn, 2, m_first, m_last)
        o_ref[pl.ds(base, GRID_W), :] = acc
        return carry

    lax.fori_loop(0, n_rows, row_body, 0)


def _conv(feat, conv_w9, geo, conv_dim):
    rows = feat.shape[0]
    nt, tc_len, t_lat = geo["NT"], geo["Tc"], geo["T"]
    tcw = _pick_tile(conv_dim, (256, 128))
    return pl.pallas_call(
        functools.partial(_conv_kernel, tc_len=tc_len, n_rows=t_lat // GRID_W),
        out_shape=jax.ShapeDtypeStruct((rows, conv_dim), F32),
        grid=(geo["B"], conv_dim // tcw),
        in_specs=[pl.BlockSpec((nt, tcw), lambda b, j: (b, j)),
                  pl.BlockSpec((9, tcw), lambda b, j: (0, j))],
        out_specs=pl.BlockSpec((nt, tcw), lambda b, j: (b, j)),
        compiler_params=_cparams(("parallel", "parallel")),
        name="dwconv",
    )(feat, conv_w9)


def _chunk_of(d, n, n_ctx, n_all):
    bwd = jnp.where(n < n_ctx, n_ctx - 1 - n, n_all + n_ctx - 1 - n)
    return jnp.where(d == 0, n, bwd)


def _order_masks(d, size):
    row = lax.broadcasted_iota(jnp.int32, (size, size), 0)
    col = lax.broadcasted_iota(jnp.int32, (size, size), 1)
    rel = (col - row) * jnp.where(d == 0, 1, -1)
    return rel <= 0, rel < 0


def _col_of_row(row_vec):
    w = row_vec.shape[1]
    return jnp.broadcast_to(row_vec, (SUBLANE, w)).T[:, 0:1]


def _rwkv_kernel(r_ref, k_ref, v_ref, dw_ref, da_ref, w0_ref, w2_ref, a0_ref, a2_ref,
                 kk_ref, ka_ref, rk_ref, gnw_ref, gnb_ref, z_ref, st_ref, *, hp):
    d = pl.program_id(0)
    n = pl.program_id(3)
    hd = A_HEAD_DIM
    rank = w2_ref.shape[0]

    @pl.when(n == 0)
    def _():
        st_ref[...] = jnp.zeros_like(st_ref)

    incl, strict = _order_masks(d, CHUNK)
    fwd = d == 0
    r = r_ref[...]
    k = k_ref[...]
    v = v_ref[...]
    dw = dw_ref[...]
    da = da_ref[...]
    dwd = jnp.where(fwd, dw[:, 0:rank], dw[:, rank:2 * rank])
    dad = jnp.where(fwd, da[:, 0:rank], da[:, rank:2 * rank])
    logw = -A_DECAY_SCALE * _sigmoid(w0_ref[...] + _dot(jnp.tanh(dwd), w2_ref[...]))
    a = _sigmoid(a0_ref[...] + _dot(dad, a2_ref[...]))
    kks = k * kk_ref[...]
    kd = k * (1.0 + (a - 1.0) * ka_ref[...])
    cg = _dot_exact_lhs(incl.astype(F32), logw)
    tot = jnp.where(fwd, cg[CHUNK - 1:CHUNK, :], cg[0:1, :])
    e_pos = jnp.exp(cg)
    e_neg = jnp.exp(-cg)
    e_prev = jnp.exp(cg - logw)
    e_end = jnp.exp(tot - cg)
    g_end_col = jnp.exp(_col_of_row(tot))
    eye = (lax.broadcasted_iota(jnp.int32, (CHUNK, CHUNK), 0)
           == lax.broadcasted_iota(jnp.int32, (CHUNK, CHUNK), 1)).astype(F32)
    rkr = r * rk_ref[...]

    for h in range(hp):
        sl = slice(h * hd, (h + 1) * hd)
        kk_h = kks[:, sl]
        kk_h = kk_h * lax.rsqrt(jnp.sum(kk_h * kk_h, axis=-1, keepdims=True) + 1e-12)
        q_h = kk_h * a[:, sl]
        pt = kk_h * e_prev[:, sl]
        qt = q_h * e_neg[:, sl]
        kt = kd[:, sl] * e_neg[:, sl]
        rt = r[:, sl] * e_pos[:, sl]
        v_h = v[:, sl]
        st = st_ref[sl, :]
        a_pq = jnp.where(strict, _dot2(pt, qt, _NT), 0.0)
        a_pk = jnp.where(strict, _dot2(pt, kt, _NT), 0.0)
        a_rq = jnp.where(incl, _dot2(rt, qt, _NT), 0.0)
        a_rk = jnp.where(incl, _dot2(rt, kt, _NT), 0.0)
        pw = -a_pq
        tinv = eye + pw
        for _ in range(5):
            pw = _dot2(pw, pw)
            tinv = tinv + _dot2(tinv, pw)
        u = _dot2(tinv, _dot2(pt, st) + _dot2(a_pk, v_h))
        y = _dot2(rt, st) + _dot2(a_rk, v_h) - _dot2(a_rq, u)
        kend = kd[:, sl] * e_end[:, sl]
        qend = q_h * e_end[:, sl]
        st_ref[sl, :] = st * g_end_col[sl, :] + _dot2(kend, v_h, _TN) - _dot2(qend, u, _TN)
        mu = jnp.mean(y, axis=-1, keepdims=True)
        yc = y - mu
        var = jnp.mean(yc * yc, axis=-1, keepdims=True)
        yn = yc * lax.rsqrt(var + A_GN_EPS) * gnw_ref[:, sl] + gnb_ref[:, sl]
        bonus = jnp.sum(rkr[:, sl] * kd[:, sl], axis=-1, keepdims=True) * v_h
        z_ref[:, sl] = yn + bonus


def _rwkv(cv, feat, p, geo, a_dim, off_dw):
    rows = cv.shape[0]
    hp = 4
    w = hp * A_HEAD_DIM
    ng = a_dim // w
    nch, ncc = geo["NCH"], geo["NCC"]
    rank = p["w2"].shape[1]
    cb = a_dim // w

    def tok(sec):
        return pl.BlockSpec((CHUNK, w), lambda d, b, g, n: (b * nch + _chunk_of(d, n, ncc, nch), sec * cb + g))

    def lora(col_block):
        return pl.BlockSpec((CHUNK, LANE), lambda d, b, g, n: (b * nch + _chunk_of(d, n, ncc, nch), col_block))

    def per_dir(shape):
        return pl.BlockSpec((None,) + shape, lambda d, b, g, n: (d, 0, g))

    def shared():
        return pl.BlockSpec((1, w), lambda d, b, g, n: (0, g))

    return pl.pallas_call(
        functools.partial(_rwkv_kernel, hp=hp),
        out_shape=jax.ShapeDtypeStruct((2, rows, a_dim), F32),
        grid=(2, geo["B"], ng, nch),
        in_specs=[tok(0), tok(1), tok(2), lora(off_dw // LANE), lora(off_dw // LANE + 1),
                  per_dir((1, w)), per_dir((rank, w)), per_dir((1, w)), per_dir((rank, w)),
                  shared(), shared(), shared(), shared(), shared()],
        out_specs=pl.BlockSpec((None, CHUNK, w),
                               lambda d, b, g, n: (d, b * nch + _chunk_of(d, n, ncc, nch), g)),
        scratch_shapes=[pltpu.VMEM((w, A_HEAD_DIM), F32)],
        compiler_params=_cparams(("parallel", "parallel", "parallel", "arbitrary")),
        name="rwkv7_scan",
    )(cv, cv, cv, feat, feat, p["w0"], p["w2"], p["a0"], p["a2"],
      p["k_k"], p["k_a"], p["r_k"], p["gn_w"], p["gn_b"])


def _gla_kernel(q_ref, k_ref, v_ref, dg_ref, a2_ref, ab_ref, o_ref, st_ref, *, hk, hv):
    d = pl.program_id(0)
    n = pl.program_id(2)
    rank = a2_ref.shape[0]

    @pl.when(n == 0)
    def _():
        st_ref[...] = jnp.zeros_like(st_ref)

    incl, _ = _order_masks(d, CHUNK)
    fwd = d == 0
    dg = dg_ref[...]
    dgd = jnp.where(fwd, dg[:, 0:rank], dg[:, rank:2 * rank])
    zlog = _dot(dgd, a2_ref[...]) + ab_ref[...]
    g = (jnp.minimum(zlog, 0.0) - jnp.log(1.0 + jnp.exp(-jnp.abs(zlog)))) / GLA_TAU
    cg = _dot_exact_lhs(incl.astype(F32), g)
    tot = jnp.where(fwd, cg[CHUNK - 1:CHUNK, :], cg[0:1, :])
    ref = jnp.where(fwd, cg[CHUNK // 2 - 1:CHUNK // 2, :], cg[CHUNK // 2:CHUNK // 2 + 1, :])
    q = q_ref[...] * (hk ** -0.5)
    k = k_ref[...]
    v = v_ref[...]
    q_in = q * jnp.exp(cg)
    q_sc = q * jnp.exp(cg - ref)
    k_sc = k * jnp.exp(ref - cg)
    k_end = k * jnp.exp(tot - cg)
    g_end_col = jnp.exp(_col_of_row(tot))

    for h in range(B_HEADS):
        ks = slice(h * hk, (h + 1) * hk)
        vs = slice(h * hv, (h + 1) * hv)
        st = st_ref[ks, :]
        v_h = v[:, vs]
        scores = jnp.where(incl, _dot_nt(q_sc[:, ks], k_sc[:, ks]), 0.0)
        o_ref[:, vs] = _dot(q_in[:, ks], st) + _dot(scores, v_h)
        st_ref[ks, :] = st * g_end_col[ks, :] + _dot_tn(k_end[:, ks], v_h)


def _gla(cv, feat, p, geo, key_dim, val_dim, off_q, off_k, off_v, off_dg):
    rows = cv.shape[0]
    nch, ncc = geo["NCH"], geo["NCC"]
    hk, hv = key_dim // B_HEADS, val_dim // B_HEADS
    rank = p["a2"].shape[1]

    def tok(width, off):
        return pl.BlockSpec((CHUNK, width),
                            lambda d, b, n: (b * nch + _chunk_of(d, n, ncc, nch), off // width))

    return pl.pallas_call(
        functools.partial(_gla_kernel, hk=hk, hv=hv),
        out_shape=jax.ShapeDtypeStruct((2, rows, val_dim), F32),
        grid=(2, geo["B"], nch),
        in_specs=[tok(key_dim, off_q), tok(key_dim, off_k), tok(val_dim, off_v), tok(LANE, off_dg),
                  pl.BlockSpec((None, rank, key_dim), lambda d, b, n: (d, 0, 0)),
                  pl.BlockSpec((None, 1, key_dim), lambda d, b, n: (d, 0, 0))],
        out_specs=pl.BlockSpec((None, CHUNK, val_dim),
                               lambda d, b, n: (d, b * nch + _chunk_of(d, n, ncc, nch), 0)),
        scratch_shapes=[pltpu.VMEM((key_dim, hv), F32)],
        compiler_params=_cparams(("parallel", "parallel", "arbitrary")),
        name="gla_scan",
    )(cv, cv, cv, feat, p["a2"], p["ab"])


def _s5_kernel(uf_ref, ub_ref, wb_ref, wc_ref, sc_ref, yf_ref, yb_ref, buf_ref, bub_ref, car_ref, *, ts, nj):
    n = pl.program_id(1)
    half = wb_ref.shape[-1] // 2
    nblk = ts // SUBLANE

    @pl.when(n == 0)
    def _():
        car_ref[...] = jnp.zeros_like(car_ref)

    for j in range(nj):
        cs = slice(j * LANE, (j + 1) * LANE)
        buf_ref[...] = _dot(uf_ref[:, cs], wb_ref[0, j])
        bub_ref[...] = _dot(ub_ref[:, cs], wb_ref[1, j])
        consts = [[(sc_ref[dd, j, kind, 0], sc_ref[dd, j, kind, 1]) for kind in range(4)] for dd in range(2)]

        def blk(i, carry, j=j, consts=consts):
            out = []
            for dd, ref in ((0, buf_ref), (1, bub_ref)):
                cr, ci = carry[2 * dd], carry[2 * dd + 1]
                off = pl.multiple_of((i if dd == 0 else nblk - 1 - i) * SUBLANE, SUBLANE)
                br = ref[pl.ds(off, SUBLANE), 0:half]
                bi = ref[pl.ds(off, SUBLANE), half:2 * half]
                for lev, s in enumerate((1, 2, 4)):
                    lr, li = consts[dd][lev]
                    sh = s if dd == 0 else SUBLANE - s
                    rr = pltpu.roll(br, sh, 0)
                    ri = pltpu.roll(bi, sh, 0)
                    br, bi = br + lr * rr - li * ri, bi + lr * ri + li * rr
                car, cai = consts[dd][3]
                xr = br + car * cr - cai * ci
                xi = bi + car * ci + cai * cr
                ref[pl.ds(off, SUBLANE), 0:half] = xr
                ref[pl.ds(off, SUBLANE), half:2 * half] = xi
                last = SUBLANE - 1 if dd == 0 else 0
                out += [xr[last:last + 1, :], xi[last:last + 1, :]]
            return tuple(out)

        init = tuple(car_ref[dd, j, c:c + 1, :] for dd in range(2) for c in range(2))
        fin = lax.fori_loop(0, nblk, blk, init)
        for dd in range(2):
            for c in range(2):
                car_ref[dd, j, c:c + 1, :] = fin[2 * dd + c]
        yf_ref[:, cs] = _dot(buf_ref[...], wc_ref[j])
        yb_ref[:, cs] = _dot(bub_ref[...], wc_ref[j])


def _s5(feat, consts, geo, c_dim, off_u):
    rows = feat.shape[0]
    ts = geo["TS"]
    nch, ncc = geo["NT"] // ts, geo["Tc"] // ts
    nj = c_dim // LANE
    half = consts["wb"].shape[-1] // 2
    ub = off_u // c_dim

    def tok(dd):
        return pl.BlockSpec((ts, c_dim), lambda b, n: (b * nch + _chunk_of(dd, n, ncc, nch), ub))

    def out(dd):
        return pl.BlockSpec((ts, c_dim), lambda b, n: (b * nch + _chunk_of(dd, n, ncc, nch), 0))

    full = lambda a: pl.BlockSpec(a.shape, lambda b, n: (0,) * a.ndim)
    return pl.pallas_call(
        functools.partial(_s5_kernel, ts=ts, nj=nj),
        out_shape=[jax.ShapeDtypeStruct((rows, c_dim), F32)] * 2,
        grid=(geo["B"], nch),
        in_specs=[tok(0), tok(1), full(consts["wb"]), full(consts["wc"]), full(consts["sc"])],
        out_specs=[out(0), out(1)],
        scratch_shapes=[pltpu.VMEM((ts, 2 * half), F32), pltpu.VMEM((ts, 2 * half), F32),
                        pltpu.VMEM((2, nj, 2, half), F32)],
        compiler_params=_cparams(("parallel", "arbitrary")),
        name="s5_scan",
    )(feat, feat, consts["wb"], consts["wc"], consts["sc"])


def _s5_consts(lam_re, lam_im, log_dt, b_re, b_im, c_re, c_im):
    ng, ns = lam_re.shape[1], lam_re.shape[2]
    gpl = LANE // C_GROUP
    nj = ng // gpl
    dt = jnp.exp(log_dt)[..., None]
    den = lam_re * lam_re + lam_im * lam_im

    def apow(m):
        mag = jnp.exp(lam_re * dt * m)
        return mag * jnp.cos(lam_im * dt * m), mag * jnp.sin(lam_im * dt * m)

    ar, ai = apow(1.0)
    fr = ((ar - 1.0) * lam_re + ai * lam_im) / den
    fi = (ai * lam_re - (ar - 1.0) * lam_im) / den
    bbr = fr[..., None] * b_re - fi[..., None] * b_im
    bbi = fr[..., None] * b_im + fi[..., None] * b_re
    eye = jnp.eye(gpl, dtype=F32)

    def in_map(bb):
        bb = bb.reshape(2, nj, gpl, ns, C_GROUP)
        return jnp.einsum("djgpc,gh->djgchp", bb, eye).reshape(2, nj, gpl * C_GROUP, gpl * ns)

    wb = jnp.concatenate([in_map(bbr), in_map(bbi)], axis=-1).astype(BF16)

    def out_map(cc):
        cc = cc.reshape(nj, gpl, C_GROUP, ns)
        return jnp.einsum("jgcp,gh->jgphc", cc, eye).reshape(nj, gpl * ns, gpl * C_GROUP)

    wc = jnp.concatenate([out_map(c_re), out_map(-c_im)], axis=1).astype(BF16)

    t = jnp.arange(SUBLANE, dtype=F32)
    per_dir = []
    for dd in range(2):
        expo, mask = [], []
        for s in (1, 2, 4):
            expo.append(jnp.full((SUBLANE,), float(s), F32))
            mask.append(((t >= s) if dd == 0 else (t <= SUBLANE - 1 - s)).astype(F32))
        expo.append(t + 1.0 if dd == 0 else SUBLANE - t)
        mask.append(jnp.ones((SUBLANE,), F32))
        expo = jnp.stack(expo)[:, :, None, None]
        mask = jnp.stack(mask)[:, :, None, None]
        zr = (lam_re[dd] * dt[dd])[None, None]
        zi = (lam_im[dd] * dt[dd])[None, None]
        mag = jnp.exp(zr * expo) * mask
        both = jnp.stack([mag * jnp.cos(zi * expo), mag * jnp.sin(zi * expo)], axis=1)
        both = both.reshape(4, 2, SUBLANE, nj, gpl * ns)
        per_dir.append(jnp.moveaxis(both, 3, 0))
    sc = jnp.stack(per_dir, axis=0)
    return {"wb": wb, "wc": wc, "sc": sc}


def _merge1_kernel(z_ref, o_ref, yf_ref, yb_ref, dg_ref, og_ref, u_ref, gates_ref,
                   g2_ref, gng_ref, sd_ref, glu_ref, pa_ref, pb_ref, pc_ref, m_ref, *, hv, d_model):
    ya = (z_ref[0] + z_ref[1]) * _dot(_sigmoid(dg_ref[...]), g2_ref[...])
    o = o_ref[0] + o_ref[1]
    og = og_ref[...]
    parts = []
    for h in range(B_HEADS):
        vs = slice(h * hv, (h + 1) * hv)
        oh = o[:, vs]
        oh = oh * lax.rsqrt(jnp.mean(oh * oh, axis=-1, keepdims=True) + NORM_EPS) * gng_ref[...]
        parts.append(_dot(oh * _silu(og[:, vs]), pb_ref[vs, :]))
    mb = parts[0] + parts[1] + parts[2] + parts[3]
    u = u_ref[...]
    yc = yf_ref[...] + yb_ref[...] + sd_ref[...] * u
    yc = 0.5 * yc * (1.0 + jnp.tanh(math.sqrt(2.0 / math.pi) * (yc + 0.044715 * yc * yc * yc)))
    gl = _dot(yc, glu_ref[...])
    cdim = u.shape[1]
    yc = gl[:, 0:cdim] * _sigmoid(gl[:, cdim:2 * cdim])
    gts = gates_ref[...].astype(F32)
    ga = _sigmoid(gts[:, 0:d_model])
    gb = _sigmoid(gts[:, d_model:2 * d_model])
    gc = _sigmoid(gts[:, 2 * d_model:3 * d_model])
    m = ga * _dot(ya, pa_ref[...]) + gb * mb + gc * _dot(yc, pc_ref[...])
    m_ref[...] = m.astype(BF16)


def _merge1(z, o, yf, yb, feat, gates, p, geo, dims):
    rows = feat.shape[0]
    d = dims["D"]
    a_dim, b_dim, c_dim = dims["A"], dims["Bv"], dims["C"]
    tm = _pick_tile(rows, (256, 128, 64))
    full = lambda a: pl.BlockSpec(a.shape, lambda i: (0,) * a.ndim)
    return pl.pallas_call(
        functools.partial(_merge1_kernel, hv=b_dim // B_HEADS, d_model=d),
        out_shape=jax.ShapeDtypeStruct((rows, d), BF16),
        grid=(rows // tm,),
        in_specs=[pl.BlockSpec((2, tm, a_dim), lambda i: (0, i, 0)),
                  pl.BlockSpec((2, tm, b_dim), lambda i: (0, i, 0)),
                  pl.BlockSpec((tm, c_dim), lambda i: (i, 0)),
                  pl.BlockSpec((tm, c_dim), lambda i: (i, 0)),
                  pl.BlockSpec((tm, LANE), lambda i: (i, dims["off_dgate"] // LANE)),
                  pl.BlockSpec((tm, b_dim), lambda i: (i, dims["off_og"] // b_dim)),
                  pl.BlockSpec((tm, c_dim), lambda i: (i, dims["off_u"] // c_dim)),
                  pl.BlockSpec((tm, 3 * d), lambda i: (i, 0)),
                  full(p["g2"]), full(p["gla_norm_g"]), full(p["s5_d"]), full(p["glu_w"]),
                  full(p["pa"]), full(p["pb"]), full(p["pc"])],
        out_specs=pl.BlockSpec((tm, d), lambda i: (i, 0)),
        compiler_params=_cparams(("parallel",)),
        name="merge_branches",
    )(z, o, yf, yb, feat, feat, feat, gates,
      p["g2"], p["gla_norm_g"], p["s5_d"], p["glu_w"], p["pa"], p["pb"], p["pc"])


def _merge2_kernel(m_ref, wo_ref, x_ref, g_ref, mod_ref, rw_ref, x1_ref, h2_ref, lg_ref):
    acc = jnp.dot(m_ref[...], wo_ref[...], preferred_element_type=F32)
    x1 = x_ref[...] + mod_ref[2:3, :] * acc
    x1_ref[...] = x1
    h2 = _norm_mod(x1, g_ref[...], mod_ref[3:4, :], mod_ref[4:5, :])
    h2_ref[...] = h2.astype(BF16)
    lg_ref[...] = _dot2(h2, rw_ref[...])


def _merge2(m, wo, xa, g2n, mods, layer, router_w_pad, geo):
    rows, d = xa.shape
    r = geo["R"]
    return pl.pallas_call(
        _merge2_kernel,
        out_shape=[jax.ShapeDtypeStruct((rows, d), F32), jax.ShapeDtypeStruct((rows, d), BF16),
                   jax.ShapeDtypeStruct((rows, LANE), F32)],
        grid=(rows // r,),
        in_specs=[pl.BlockSpec((r, d), lambda i: (i, 0)),
                  pl.BlockSpec((d, d), lambda i: (0, 0)),
                  pl.BlockSpec((r, d), lambda i: (i, 0)),
                  pl.BlockSpec((1, d), lambda i: (0, 0)),
                  _mod_spec(layer, geo["TPB"], geo["B"], d),
                  pl.BlockSpec((d, LANE), lambda i: (0, 0))],
        out_specs=[pl.BlockSpec((r, d), lambda i: (i, 0)), pl.BlockSpec((r, d), lambda i: (i, 0)),
                   pl.BlockSpec((r, LANE), lambda i: (i, 0))],
        compiler_params=_cparams(("parallel",)),
        name="out_proj_norm2",
    )(m, wo, xa, g2n.reshape(1, d), mods, router_w_pad)


def _moe_kernel(te_ref, x_ref, w1_ref, w3_ref, w2_ref, wr_ref, o_ref):
    del te_ref
    x = x_ref[...]
    h1 = jnp.dot(x, w1_ref[...], preferred_element_type=F32)
    h3 = jnp.dot(x, w3_ref[...], preferred_element_type=F32)
    he = (_silu(h1) * h3).astype(BF16)
    y = jnp.dot(he, w2_ref[...], preferred_element_type=F32)
    o_ref[...] = (y * wr_ref[...]).astype(o_ref.dtype)


def _moe_ffn(xs, w1, w3, w2, wrow, tile_expert):
    p_rows, d = xs.shape
    de = w1.shape[2]
    nt = p_rows // MOE_TILE
    return pl.pallas_call(
        _moe_kernel,
        out_shape=jax.ShapeDtypeStruct((p_rows, d), BF16),
        grid_spec=pltpu.PrefetchScalarGridSpec(
            num_scalar_prefetch=1,
            grid=(nt,),
            in_specs=[pl.BlockSpec((MOE_TILE, d), lambda i, te: (i, 0)),
                      pl.BlockSpec((None, d, de), lambda i, te: (te[i], 0, 0)),
                      pl.BlockSpec((None, d, de), lambda i, te: (te[i], 0, 0)),
                      pl.BlockSpec((None, de, d), lambda i, te: (te[i], 0, 0)),
                      pl.BlockSpec((MOE_TILE, 1), lambda i, te: (i, 0))],
            out_specs=pl.BlockSpec((MOE_TILE, d), lambda i, te: (i, 0)),
        ),
        compiler_params=_cparams(("arbitrary",)),
        name="moe_experts",
    )(tile_expert, xs, w1, w3, w2, wrow)


def _route(logits, router_bias, n_experts):
    scores = _sigmoid(logits[:, :n_experts])
    biased = scores + router_bias.astype(F32)
    epg = n_experts // N_EXPERT_GROUPS
    grp = biased.reshape(-1, N_EXPERT_GROUPS, epg)
    gsel = jnp.argmax(jnp.sum(lax.top_k(grp, TOP_K)[0], axis=-1), axis=-1)
    in_grp = (jnp.arange(N_EXPERT_GROUPS) == gsel[:, None])[..., None]
    masked = jnp.where(in_grp, grp, -jnp.inf).reshape(biased.shape)
    _, idx = lax.top_k(masked, TOP_K)
    wsel = jnp.take_along_axis(scores, idx, axis=-1)
    wsel = wsel / jnp.sum(wsel, axis=-1, keepdims=True)
    return idx.astype(jnp.int32), wsel


def _dispatch(idx, wsel, n_experts):
    n_tok = idx.shape[0]
    n_pairs = n_tok * TOP_K
    n_tiles = -(-n_pairs // MOE_TILE) + n_experts
    p_rows = n_tiles * MOE_TILE
    e_flat = idx.reshape(-1)
    order = jnp.argsort(e_flat, stable=True)
    e_sorted = e_flat[order]
    counts = jnp.zeros((n_experts,), jnp.int32).at[e_flat].add(1)
    padded = ((counts + MOE_TILE - 1) // MOE_TILE) * MOE_TILE
    gstart = jnp.cumsum(padded) - padded
    cstart = jnp.cumsum(counts) - counts
    dest_sorted = gstart[e_sorted] + (jnp.arange(n_pairs, dtype=jnp.int32) - cstart[e_sorted])
    src_tok = jnp.zeros((p_rows,), jnp.int32).at[dest_sorted].set((order // TOP_K).astype(jnp.int32))
    wrow = jnp.zeros((p_rows,), F32).at[dest_sorted].set(wsel.reshape(-1)[order])
    pos = jnp.zeros((n_pairs,), jnp.int32).at[order].set(dest_sorted).reshape(n_tok, TOP_K)
    tile_start = jnp.arange(n_tiles, dtype=jnp.int32) * MOE_TILE
    gend = gstart + padded
    tile_expert = jnp.minimum(jnp.sum((tile_start[:, None] >= gend[None, :]).astype(jnp.int32), axis=1),
                              n_experts - 1).astype(jnp.int32)
    return src_tok, wrow.reshape(p_rows, 1), pos, tile_expert


def _res_norm_kernel(x_ref, y_ref, mod_ref, g_ref, nmod_ref, x2_ref, h_ref):
    x2 = x_ref[...] + mod_ref[5:6, :] * y_ref[...]
    x2_ref[...] = x2
    h_ref[...] = _norm_mod(x2, g_ref[...], nmod_ref[0:1, :], nmod_ref[1:2, :]).astype(BF16)


def _res_norm(x1, y, mods, layer, g_next, geo):
    rows, d = x1.shape
    r = geo["R"]
    return pl.pallas_call(
        _res_norm_kernel,
        out_shape=[jax.ShapeDtypeStruct((rows, d), F32), jax.ShapeDtypeStruct((rows, d), BF16)],
        grid=(rows // r,),
        in_specs=[pl.BlockSpec((r, d), lambda i: (i, 0)), pl.BlockSpec((r, d), lambda i: (i, 0)),
                  _mod_spec(layer, geo["TPB"], geo["B"], d),
                  pl.BlockSpec((1, d), lambda i: (0, 0)),
                  _mod_spec(layer + 1, geo["TPB"], geo["B"], d)],
        out_specs=[pl.BlockSpec((r, d), lambda i: (i, 0)), pl.BlockSpec((r, d), lambda i: (i, 0))],
        compiler_params=_cparams(("parallel",)),
        name="residual_norm1",
    )(x1, y, mods, g_next.reshape(1, d), mods)


def _res_final_kernel(x_ref, y_ref, mod_ref, g_ref, o_ref):
    x2 = x_ref[...] + mod_ref[5:6, :] * y_ref[...]
    o_ref[...] = x2 * lax.rsqrt(jnp.mean(x2 * x2, axis=-1, keepdims=True) + NORM_EPS) * g_ref[...]


def _res_final(x1, y, mods, layer, g_final, geo):
    d = x1.shape[1]
    r = geo["R"]
    tpb = geo["TPB"]
    lat_tiles = tpb - 1
    n_out = geo["B"] * lat_tiles

    def src(i):
        return ((i // lat_tiles) * tpb + 1 + i % lat_tiles, 0)

    return pl.pallas_call(
        _res_final_kernel,
        out_shape=jax.ShapeDtypeStruct((n_out * r, d), F32),
        grid=(n_out,),
        in_specs=[pl.BlockSpec((r, d), src), pl.BlockSpec((r, d), src),
                  pl.BlockSpec((None, None, N_MOD, d), lambda i: (layer, i // lat_tiles, 0, 0)),
                  pl.BlockSpec((1, d), lambda i: (0, 0))],
        out_specs=pl.BlockSpec((r, d), lambda i: (i, 0)),
        compiler_params=_cparams(("parallel",)),
        name="residual_final_norm",
    )(x1, y, mods, g_final.reshape(1, d))


def kernel(x, c, ctx, c_ctx, mod_w, mod_b, norm1_g, norm2_g, w_in, conv_w, rk_w0, rk_w2, rk_a0, rk_a2, rk_g2, rk_kk, rk_ka, rk_rk, rk_gn_w, rk_gn_b, gla_a2, gla_ab, gla_norm_g, s5_lam_re, s5_lam_im, s5_log_dt, s5_b_re, s5_b_im, s5_c_re, s5_c_im, s5_d, s5_glu_w, proj_a, proj_b, proj_c, w_out, router_w, router_bias, exp_w1, exp_w3, exp_w2, final_g):
    nb, t_lat, d = x.shape
    t_ctx = ctx.shape[1]
    depth = mod_w.shape[0]
    n_experts = router_w.shape[1]
    a_dim = rk_w0.shape[-1]
    key_dim = gla_ab.shape[-1]
    b_dim = proj_b.shape[1]
    c_dim = proj_c.shape[1]
    dec_rank, iclr_rank = rk_w2.shape[2], rk_a2.shape[2]
    gate_rank, bg_rank = rk_g2.shape[1], gla_a2.shape[2]
    conv_dim = 3 * a_dim + 2 * key_dim + b_dim
    assert t_lat % t_ctx == 0 and t_ctx % CHUNK == 0 and t_lat % GRID_W == 0
    assert nb + 1 <= SUBLANE and dec_rank == iclr_rank and 2 * dec_rank == LANE and gate_rank == LANE
    assert 2 * bg_rank <= LANE and conv_dim % LANE == 0

    nt = t_ctx + t_lat
    rows = nb * nt
    geo = {"B": nb, "T": t_lat, "Tc": t_ctx, "NT": nt, "R": t_ctx, "TPB": nt // t_ctx,
           "NCH": nt // CHUNK, "NCC": t_ctx // CHUNK, "TS": 128 if t_ctx % 128 == 0 else 64}

    off_og = conv_dim
    off_u = off_og + b_dim
    off_dw = off_u + c_dim
    off_da = off_dw + 2 * dec_rank
    off_dgate = off_da + 2 * iclr_rank
    off_bg = off_dgate + gate_rank
    src_lora = conv_dim
    src_bg_end = src_lora + 2 * dec_rank + 2 * iclr_rank + gate_rank + 2 * bg_rank
    src_gates = src_bg_end + b_dim + c_dim
    assert off_og % b_dim == 0 and off_u % c_dim == 0 and off_dw % LANE == 0
    dims = {"D": d, "A": a_dim, "Bv": b_dim, "C": c_dim, "off_dgate": off_dgate, "off_og": off_og, "off_u": off_u}

    cvec = jnp.zeros((SUBLANE, d), F32).at[:nb].set(c).at[nb].set(c_ctx)
    mods = _mods(cvec, mod_w, mod_b).reshape(depth, SUBLANE, N_MOD, d)
    xa = jnp.concatenate([ctx, x], axis=1).reshape(rows, d)
    router_w_pad = jnp.zeros((d, LANE), F32).at[:, :n_experts].set(router_w)

    h = _norm1(xa, norm1_g[0], mods, 0, geo)
    out = None
    for l in range(depth):
        last = l == depth - 1
        w_l = w_in[l]
        w_feat = jnp.concatenate(
            [w_l[:, :conv_dim], w_l[:, src_bg_end:src_gates], w_l[:, src_lora:src_bg_end],
             jnp.zeros((d, LANE - 2 * bg_rank), F32)], axis=1).astype(BF16)
        w_gates = w_l[:, src_gates:].astype(BF16)
        feat = _matmul(h, w_feat, F32, "in_proj_feat")
        gates = _matmul(h, w_gates, BF16, "in_proj_gates")
        cv = _conv(feat, conv_w[l].reshape(9, conv_dim), geo, conv_dim)

        pa = {"w0": rk_w0[l].reshape(2, 1, a_dim), "w2": rk_w2[l], "a0": rk_a0[l].reshape(2, 1, a_dim),
              "a2": rk_a2[l], "k_k": rk_kk[l].reshape(1, a_dim), "k_a": rk_ka[l].reshape(1, a_dim),
              "r_k": rk_rk[l].reshape(1, a_dim), "gn_w": rk_gn_w[l].reshape(1, a_dim),
              "gn_b": rk_gn_b[l].reshape(1, a_dim)}
        z = _rwkv(cv, feat, pa, geo, a_dim, off_dw)
        pb = {"a2": gla_a2[l], "ab": gla_ab[l].reshape(2, 1, key_dim)}
        o = _gla(cv, feat, pb, geo, key_dim, b_dim, 3 * a_dim, 3 * a_dim + key_dim, 3 * a_dim + 2 * key_dim, off_bg)
        s5c = _s5_consts(s5_lam_re[l], s5_lam_im[l], s5_log_dt[l], s5_b_re[l], s5_b_im[l], s5_c_re[l], s5_c_im[l])
        yf, yb = _s5(feat, s5c, geo, c_dim, off_u)

        pm = {"g2": rk_g2[l].astype(BF16), "gla_norm_g": gla_norm_g[l].reshape(1, -1),
              "s5_d": s5_d[l].reshape(1, c_dim), "glu_w": s5_glu_w[l].astype(BF16),
              "pa": proj_a[l].astype(BF16), "pb": proj_b[l].astype(BF16), "pc": proj_c[l].astype(BF16)}
        m = _merge1(z, o, yf, yb, feat, gates, pm, geo, dims)
        x1, h2, logits = _merge2(m, w_out[l].astype(BF16), xa, norm2_g[l], mods, l, router_w_pad, geo)

        idx, wsel = _route(logits, router_bias, n_experts)
        src_tok, wrow, pos, tile_expert = _dispatch(idx, wsel, n_experts)
        xs = jnp.take(h2, src_tok, axis=0)
        ys = _moe_ffn(xs, exp_w1[l].astype(BF16), exp_w3[l].astype(BF16), exp_w2[l].astype(BF16), wrow, tile_expert)
        ytok = jnp.take(ys, pos[:, 0], axis=0).astype(F32) + jnp.take(ys, pos[:, 1], axis=0).astype(F32)
        if last:
            out = _res_final(x1, ytok, mods, l, final_g, geo)
        else:
            xa, h = _res_norm(x1, ytok, mods, l, norm1_g[l + 1], geo)
    return out.reshape(nb, t_lat, d)
```
